```python
import jax, jax.numpy as jnp
from jax import lax
import numpy as np

D_MODEL = 1024
BATCH = 16
SEQ = 4096
DEPTH = 4

GRID_W = 64
CTX_LEN = 256
N_EVEN = (DEPTH + 1) // 2
N_ODD = DEPTH // 2
N_SUB = 3
D_FF = 2816
HEAD_DIM = 64
NA_HEADS = 12
NA_WIN_ROWS = 8
NA_WIN_COLS = 16
NA_QCOL_BLOCK = 16
NA_KCOL_BLOCK = 32
NA_W = NA_HEADS * HEAD_DIM
FNET_GROUPS = 4
FNET_GROUP_DIM = 64
FNET_W = FNET_GROUPS * FNET_GROUP_DIM
IN_EVEN = 3 * NA_W + FNET_W
MIX_EVEN = NA_W + FNET_W
GQA_Q_HEADS = 8
GQA_KV_HEADS = 2
GQA_Q_W = GQA_Q_HEADS * HEAD_DIM
GQA_KV_W = GQA_KV_HEADS * HEAD_DIM
SGU_GROUPS = 4
SGU_CHUNK = 128
SGU_GROUP_DIM = 128
SGU_W = SGU_GROUPS * SGU_GROUP_DIM
IN_ODD = GQA_Q_W + 2 * GQA_KV_W + 2 * SGU_W
MIX_ODD = GQA_Q_W + SGU_W
Q_BLOCK = 128
ROPE_THETA = 10000.0
EPS = 1e-6
NEG_INF = -1e30

kernel_name = "hybrid_natten_fnet_gqa_sgu_macaron_dit"


def rms_norm(x, g):
    x32 = x.astype(jnp.float32)
    y = x32 * lax.rsqrt(jnp.mean(x32 * x32, axis=-1, keepdims=True) + EPS)
    return (y * g.astype(jnp.float32)).astype(x.dtype)


def modulate(h, shift, scale):
    return h * (1 + scale) + shift


def swiglu(h, w_gu, w_down):
    gate, up = jnp.split(h @ w_gu, 2, axis=-1)
    return (jax.nn.silu(gate) * up) @ w_down


def macaron_ffn(x, g, shift, scale, gate, w_gu, w_down):
    h = modulate(rms_norm(x, g), shift, scale)
    return x + 0.5 * gate * swiglu(h, w_gu, w_down)


def heads(z, n):
    return z.reshape(z.shape[:-1] + (n, HEAD_DIM))


def axial_rope_tables(n_tokens):
    t = jnp.arange(n_tokens, dtype=jnp.int32)
    n_freq = HEAD_DIM // 4
    inv_freq = ROPE_THETA ** (-jnp.arange(n_freq, dtype=jnp.float32) / n_freq)
    row = (t // GRID_W).astype(jnp.float32)[:, None] * inv_freq
    col = (t % GRID_W).astype(jnp.float32)[:, None] * inv_freq
    ang = jnp.concatenate([row, col], axis=-1)
    return jnp.cos(ang), jnp.sin(ang)


def apply_rope(x, cos, sin):
    xf = x.astype(jnp.float32).reshape(x.shape[:-1] + (HEAD_DIM // 2, 2))
    x0, x1 = xf[..., 0], xf[..., 1]
    c = cos[None, :, None, :]
    s = sin[None, :, None, :]
    out = jnp.stack([x0 * c - x1 * s, x0 * s + x1 * c], axis=-1)
    return out.reshape(x.shape).astype(x.dtype)


def gqa_attend(q, k, v):
    b, tq, hq, dh = q.shape
    hkv = k.shape[2]
    qg = q.reshape(b, tq, hkv, hq // hkv, dh)
    s = jnp.einsum('bqkgd,btkd->bkgqt', qg, k).astype(jnp.float32) * (dh ** -0.5)
    p = jax.nn.softmax(s, axis=-1).astype(v.dtype)
    o = jnp.einsum('bkgqt,btkd->bqkgd', p, v)
    return o.reshape(b, tq, hq * dh)


def blocked_gqa_attend(q, k, v):
    b, s, hq, dh = q.shape
    nb = s // Q_BLOCK
    qb = q.reshape(b, nb, Q_BLOCK, hq, dh).transpose(1, 0, 2, 3, 4)
    out = lax.map(lambda qi: gqa_attend(qi, k, v), qb)
    return out.transpose(1, 0, 2, 3).reshape(b, s, hq * dh)


def neighbourhood_attend(q, k, v, k_ctx, v_ctx, rpb):
    b, s, h, dh = q.shape
    rows = s // GRID_W
    kr = min(NA_WIN_ROWS, rows)
    n_cb = GRID_W // NA_QCOL_BLOCK
    qcol = np.arange(GRID_W).reshape(n_cb, NA_QCOL_BLOCK)
    col_start = np.clip(qcol - NA_WIN_COLS // 2, 0, GRID_W - NA_WIN_COLS)
    kcol0 = np.clip(np.arange(n_cb) * NA_QCOL_BLOCK - NA_WIN_COLS // 2, 0, GRID_W - NA_KCOL_BLOCK)
    kcol = kcol0[:, None] + np.arange(NA_KCOL_BLOCK)
    col_ok = ((kcol[:, None, :] >= col_start[:, :, None]) &
              (kcol[:, None, :] < col_start[:, :, None] + NA_WIN_COLS))
    dcol_idx = np.clip(kcol[:, None, :] - qcol[:, :, None] + NA_WIN_COLS - 1, 0, 2 * NA_WIN_COLS - 2)
    row_start = np.clip(np.arange(rows) - kr // 2, 0, rows - kr).astype(np.int32)
    mask = jnp.asarray(col_ok[:, :, None, :])
    bias_col = rpb[:, :, dcol_idx]
    kg = k.reshape(b, rows, GRID_W, h, dh)
    vg = v.reshape(b, rows, GRID_W, h, dh)
    qg = q.reshape(b, rows, n_cb, NA_QCOL_BLOCK, h, dh).transpose(1, 0, 2, 3, 4, 5)
    scale = dh ** -0.5
    n_loc = kr * NA_KCOL_BLOCK

    def one_row(args):
        q_row, r, r0 = args
        k_nb = lax.dynamic_slice_in_dim(kg, r0, kr, axis=1)[:, :, kcol]
        v_nb = lax.dynamic_slice_in_dim(vg, r0, kr, axis=1)[:, :, kcol]
        s_loc = jnp.einsum('bjqhd,brjkhd->bhjqrk', q_row, k_nb).astype(jnp.float32) * scale
        drow_idx = r0 + jnp.arange(kr, dtype=jnp.int32) - r + NA_WIN_ROWS - 1
        bias = bias_col[:, drow_idx].transpose(0, 2, 3, 1, 4)
        s_loc = jnp.where(mask, s_loc + bias[None].astype(jnp.float32), NEG_INF)
        s_ctx = jnp.einsum('bjqhd,bthd->bhjqt', q_row, k_ctx).astype(jnp.float32) * scale
        sc = jnp.concatenate([s_loc.reshape(b, h, n_cb, NA_QCOL_BLOCK, n_loc), s_ctx], axis=-1)
        p = jax.nn.softmax(sc, axis=-1).astype(v.dtype)
        p_loc = p[..., :n_loc].reshape(b, h, n_cb, NA_QCOL_BLOCK, kr, NA_KCOL_BLOCK)
        o = (jnp.einsum('bhjqrk,brjkhd->bjqhd', p_loc, v_nb)
             + jnp.einsum('bhjqt,bthd->bjqhd', p[..., n_loc:], v_ctx))
        return o.reshape(b, GRID_W, h * dh)

    out = lax.map(one_row, (qg, jnp.arange(rows, dtype=jnp.int32), jnp.asarray(row_start)))
    return out.transpose(1, 0, 2, 3).reshape(b, s, h * dh)


def fourier_mix(f):
    b, t, _ = f.shape
    z = f.astype(jnp.float32).reshape(b, t, FNET_GROUPS, FNET_GROUP_DIM)
    y = jnp.fft.fft2(z, axes=(1, 3), norm="ortho").real
    return y.reshape(b, t, FNET_W).astype(f.dtype)


def spatial_gating(u, v, v_g, w_s, b_s):
    b, t, _ = u.shape
    u = jax.nn.gelu(u)
    v = rms_norm(jax.nn.gelu(v).reshape(b, t, SGU_GROUPS, SGU_GROUP_DIM),
                 v_g.reshape(SGU_GROUPS, SGU_GROUP_DIM))
    vc = v.reshape(b, t // SGU_CHUNK, SGU_CHUNK, SGU_GROUPS, SGU_GROUP_DIM)
    mixed = jnp.einsum('gij,bnjgc->bnigc', w_s, vc) + b_s.T[None, None, :, :, None]
    return u * mixed.reshape(b, t, SGU_W)


def mixer_ab(hl, hc, w_in, w_out, qk_g, rpb, with_ctx_out):
    shp_l = hl.shape[:2] + (NA_HEADS, HEAD_DIM)
    ql, kl, vl, fl = jnp.split(hl @ w_in, [NA_W, 2 * NA_W, 3 * NA_W], axis=-1)
    ql = rms_norm(ql.reshape(shp_l), qk_g[0])
    kl = rms_norm(kl.reshape(shp_l), qk_g[1])
    vl = vl.reshape(shp_l)
    if with_ctx_out:
        qc, kc, vc, fc = jnp.split(hc @ w_in, [NA_W, 2 * NA_W, 3 * NA_W], axis=-1)
    else:
        kc, vc = jnp.split(hc @ w_in[:, NA_W:3 * NA_W], 2, axis=-1)
    kc = rms_norm(heads(kc, NA_HEADS), qk_g[1])
    vc = heads(vc, NA_HEADS)
    att_l = neighbourhood_attend(ql, kl, vl, kc, vc, rpb)
    out_l = jnp.concatenate([att_l, fourier_mix(fl)], axis=-1) @ w_out
    if not with_ctx_out:
        return out_l, None
    qc = rms_norm(heads(qc, NA_HEADS), qk_g[0])
    att_c = gqa_attend(qc, kc, vc)
    out_c = jnp.concatenate([att_c, fourier_mix(fc)], axis=-1) @ w_out
    return out_l, out_c


def mixer_cd(hl, hc, w_in, w_out, qk_g, v_g, w_s, b_s, with_ctx_out):
    o_k = GQA_Q_W
    o_v = o_k + GQA_KV_W
    o_u = o_v + GQA_KV_W
    o_s = o_u + SGU_W
    ql, kl, val_l, ul, sl = jnp.split(hl @ w_in, [o_k, o_v, o_u, o_s], axis=-1)
    cos, sin = axial_rope_tables(hl.shape[1])
    ql = apply_rope(rms_norm(heads(ql, GQA_Q_HEADS), qk_g[0]), cos, sin)
    kl = apply_rope(rms_norm(heads(kl, GQA_KV_HEADS), qk_g[1]), cos, sin)
    val_l = heads(val_l, GQA_KV_HEADS)
    if with_ctx_out:
        qc, kc, val_c, uc, sc = jnp.split(hc @ w_in, [o_k, o_v, o_u, o_s], axis=-1)
    else:
        kc, val_c = jnp.split(hc @ w_in[:, o_k:o_u], 2, axis=-1)
    kc = rms_norm(heads(kc, GQA_KV_HEADS), qk_g[1])
    val_c = heads(val_c, GQA_KV_HEADS)
    att_l = blocked_gqa_attend(ql, jnp.concatenate([kl, kc], axis=1),
                               jnp.concatenate([val_l, val_c], axis=1))
    out_l = jnp.concatenate([att_l, spatial_gating(ul, sl, v_g, w_s, b_s)], axis=-1) @ w_out
    if not with_ctx_out:
        return out_l, None
    qc = rms_norm(heads(qc, GQA_Q_HEADS), qk_g[0])
    att_c = gqa_attend(qc, kc, val_c)
    out_c = jnp.concatenate([att_c, spatial_gating(uc, sc, v_g, w_s, b_s)], axis=-1) @ w_out
    return out_l, out_c


def setup_inputs(seed: int = 0) -> dict:
    key = jax.random.key(seed)
    ks = jax.random.split(key, 20)
    D = D_MODEL

    def nrm(k, shape, scale):
        return jax.random.normal(k, shape, jnp.float32) * scale

    return {
        "x": nrm(ks[0], (BATCH, SEQ, D), 1.0),
        "c": nrm(ks[1], (BATCH, D), 1.0),
        "ctx": nrm(ks[2], (BATCH, CTX_LEN, D), 1.0),
        "c_ctx": nrm(ks[3], (D,), 1.0),
        "norm_g": 1.0 + nrm(ks[4], (DEPTH, N_SUB, D), 0.02),
        "w_mod": nrm(ks[5], (DEPTH, D, N_SUB * 3 * D), 0.5 * D ** -0.5),
        "b_mod": nrm(ks[6], (DEPTH, N_SUB * 3 * D), 0.02),
        "ffn_w_gu": nrm(ks[7], (DEPTH, 2, D, 2 * D_FF), D ** -0.5),
        "ffn_w_down": nrm(ks[8], (DEPTH, 2, D_FF, D), D_FF ** -0.5),
        "w_in_ab": nrm(ks[9], (N_EVEN, D, IN_EVEN), D ** -0.5),
        "w_out_ab": nrm(ks[10], (N_EVEN, MIX_EVEN, D), MIX_EVEN ** -0.5),
        "qk_g_a": 1.0 + nrm(ks[11], (N_EVEN, 2, HEAD_DIM), 0.02),
        "rpb_a": nrm(ks[12], (N_EVEN, NA_HEADS, 2 * NA_WIN_ROWS - 1, 2 * NA_WIN_COLS - 1), 0.05),
        "w_in_cd": nrm(ks[13], (N_ODD, D, IN_ODD), D ** -0.5),
        "w_out_cd": nrm(ks[14], (N_ODD, MIX_ODD, D), MIX_ODD ** -0.5),
        "qk_g_d": 1.0 + nrm(ks[15], (N_ODD, 2, HEAD_DIM), 0.02),
        "v_g_c": 1.0 + nrm(ks[16], (N_ODD, SGU_W), 0.02),
        "w_s_c": nrm(ks[17], (N_ODD, SGU_GROUPS, SGU_CHUNK, SGU_CHUNK), SGU_CHUNK ** -0.5),
        "b_s_c": 1.0 + nrm(ks[18], (N_ODD, SGU_GROUPS, SGU_CHUNK), 0.02),
    }


def reference(x, c, ctx, c_ctx, norm_g, w_mod, b_mod, ffn_w_gu, ffn_w_down,
              w_in_ab, w_out_ab, qk_g_a, rpb_a, w_in_cd, w_out_cd, qk_g_d, v_g_c, w_s_c, b_s_c):
    b = x.shape[0]
    xl, xc = x, ctx
    silu_c = jax.nn.silu(c)
    silu_cc = jax.nn.silu(c_ctx)
    for layer in range(DEPTH):
        mod_l = (silu_c @ w_mod[layer] + b_mod[layer]).reshape(b, N_SUB, 3, 1, D_MODEL)
        mod_c = (silu_cc @ w_mod[layer] + b_mod[layer]).reshape(N_SUB, 3, D_MODEL)
        with_ctx_out = layer < DEPTH - 1
        xl = macaron_ffn(xl, norm_g[layer, 0], mod_l[:, 0, 0], mod_l[:, 0, 1], mod_l[:, 0, 2],
                         ffn_w_gu[layer, 0], ffn_w_down[layer, 0])
        xc = macaron_ffn(xc, norm_g[layer, 0], mod_c[0, 0], mod_c[0, 1], mod_c[0, 2],
                         ffn_w_gu[layer, 0], ffn_w_down[layer, 0])
        hl = modulate(rms_norm(xl, norm_g[layer, 1]), mod_l[:, 1, 0], mod_l[:, 1, 1])
        hc = modulate(rms_norm(xc, norm_g[layer, 1]), mod_c[1, 0], mod_c[1, 1])
        if layer % 2 == 0:
            i = layer // 2
            out_l, out_c = mixer_ab(hl, hc, w_in_ab[i], w_out_ab[i], qk_g_a[i], rpb_a[i], with_ctx_out)
        else:
            i = layer // 2
            out_l, out_c = mixer_cd(hl, hc, w_in_cd[i], w_out_cd[i], qk_g_d[i], v_g_c[i],
                                    w_s_c[i], b_s_c[i], with_ctx_out)
        xl = xl + mod_l[:, 1, 2] * out_l
        xl = macaron_ffn(xl, norm_g[layer, 2], mod_l[:, 2, 0], mod_l[:, 2, 1], mod_l[:, 2, 2],
                         ffn_w_gu[layer, 1], ffn_w_down[layer, 1])
        if with_ctx_out:
            xc = xc + mod_c[1, 2] * out_c
            xc = macaron_ffn(xc, norm_g[layer, 2], mod_c[2, 0], mod_c[2, 1], mod_c[2, 2],
                             ffn_w_gu[layer, 1], ffn_w_down[layer, 1])
    return xl
```

```python
import functools

import numpy as np
import jax
import jax.numpy as jnp
from jax import lax
from jax.experimental import pallas as pl
from jax.experimental.pallas import tpu as pltpu

F32 = jnp.float32
MXU_DTYPE = jnp.bfloat16

D_MODEL = 1024
GRID_W = 64
N_SUB = 3
HEAD_DIM = 64
LANES = 128
NA_HEADS = 12
NA_WIN_ROWS = 8
NA_WIN_COLS = 16
NA_W = NA_HEADS * HEAD_DIM
FNET_W = 256
FNET_GROUP_DIM = 64
GQA_Q_HEADS = 8
GQA_KV_HEADS = 2
GQA_Q_W = GQA_Q_HEADS * HEAD_DIM
GQA_KV_W = GQA_KV_HEADS * HEAD_DIM
SGU_GROUPS = 4
SGU_CHUNK = 128
SGU_W = 512
ROPE_THETA = 10000.0
EPS = 1e-6
NEG_INF = -1e30
ATTN_SCALE = HEAD_DIM ** -0.5

NA_QROWS = 4
NA_KROWS = 12

VMEM_LIMIT = 56 * 1024 * 1024


def _cparams(n_axes):
    return pltpu.CompilerParams(dimension_semantics=("arbitrary",) * n_axes,
                                vmem_limit_bytes=VMEM_LIMIT)


def _const_spec(shape):
    nd = len(shape)
    return pl.BlockSpec(shape, lambda *_: (0,) * nd, pipeline_mode=pl.Buffered(1))


def _silu(x):
    return x * (1.0 / (1.0 + jnp.exp(-x)))


def _gelu_tanh(x):
    return 0.5 * x * (1.0 + jnp.tanh(0.7978845608028654 * (x + 0.044715 * (x * x * x))))


def _norm_modulate(x, g, mod):
    y = x * lax.rsqrt(jnp.mean(x * x, axis=-1, keepdims=True) + EPS) * g
    return y * (1.0 + mod[1:2, :]) + mod[0:1, :]


def _dot(a, b):
    return jnp.dot(a, b, preferred_element_type=F32)


def _dot_nt(a, b):
    return lax.dot_general(a, b, (((1,), (1,)), ((), ())), preferred_element_type=F32)


def _mod_kernel(c_ref, w_ref, b_ref, o_ref):
    sc = _silu(c_ref[...]).astype(MXU_DTYPE)
    o_ref[0] = _dot(sc, w_ref[0].astype(MXU_DTYPE)) + b_ref[0]


def _adaln_params(c_rows, w_mod, b_mod):
    depth, d, n = w_mod.shape
    r = c_rows.shape[0]
    tn = 1536
    return pl.pallas_call(
        _mod_kernel,
        out_shape=jax.ShapeDtypeStruct((depth, r, n), F32),
        grid=(depth, n // tn),
        in_specs=[pl.BlockSpec((r, d), lambda l, j: (0, 0)),
                  pl.BlockSpec((1, d, tn), lambda l, j: (l, 0, j)),
                  pl.BlockSpec((1, 1, tn), lambda l, j: (l, 0, j))],
        out_specs=pl.BlockSpec((1, r, tn), lambda l, j: (l, 0, j)),
        compiler_params=_cparams(2),
        name="adaln_params",
    )(c_rows, w_mod, b_mod.reshape(depth, 1, n))


def _ffn_kernel(x_ref, mod_ref, g_ref, wgu_ref, wd_ref, o_ref):
    x = x_ref[...]
    mod = mod_ref[0]
    h = _norm_modulate(x, g_ref[...], mod).astype(MXU_DTYPE)
    gu = _dot(h, wgu_ref[...])
    dff = gu.shape[1] // 2
    a = (_silu(gu[:, :dff]) * gu[:, dff:]).astype(MXU_DTYPE)
    o_ref[...] = x + (0.5 * mod[2:3, :]) * _dot(a, wd_ref[...])


def _ffn(x, mod, g, wgu, wd, rows_per_mod, tm):
    t, d = x.shape
    tpm = rows_per_mod // tm
    return pl.pallas_call(
        _ffn_kernel,
        out_shape=jax.ShapeDtypeStruct((t, d), F32),
        grid=(t // tm,),
        in_specs=[pl.BlockSpec((tm, d), lambda i: (i, 0)),
                  pl.BlockSpec((1, 3, d), lambda i: (i // tpm, 0, 0)),
                  _const_spec((1, d)),
                  _const_spec(wgu.shape),
                  _const_spec(wd.shape)],
        out_specs=pl.BlockSpec((tm, d), lambda i: (i, 0)),
        compiler_params=_cparams(1),
        name="macaron_ffn",
    )(x, mod, g.reshape(1, d), wgu, wd)


def _head_ms(t, p_ref):
    w = t.shape[1]
    return _dot((t * t).astype(MXU_DTYPE), p_ref[:w, :w])


def _inproj_even_kernel(x_ref, mod_ref, g_ref, w_ref, gq_ref, gk_ref, p_ref, csc_ref,
                        q_ref, k_ref, v_ref, uv_ref):
    h = _norm_modulate(x_ref[...], g_ref[...], mod_ref[0]).astype(MXU_DTYPE)
    z = _dot(h, w_ref[...])
    for c in range(NA_W // 256):
        sl = slice(c * 256, (c + 1) * 256)
        q = z[:, c * 256:(c + 1) * 256]
        k = z[:, NA_W + c * 256:NA_W + (c + 1) * 256]
        q_ref[:, sl] = (q * lax.rsqrt(_head_ms(q, p_ref) + EPS) * gq_ref[:, sl]).astype(q_ref.dtype)
        k_ref[:, sl] = (k * lax.rsqrt(_head_ms(k, p_ref) + EPS) * gk_ref[:, sl]).astype(k_ref.dtype)
    v_ref[...] = z[:, 2 * NA_W:3 * NA_W].astype(v_ref.dtype)
    uv_ref[...] = _dot(z[:, 3 * NA_W:].astype(MXU_DTYPE), csc_ref[...]).astype(uv_ref.dtype)


def _inproj_even(x, mod, g, w_in, gq, gk, p256, csc, rows_per_mod, tm):
    t, d = x.shape
    tpm = rows_per_mod // tm
    row = lambda i: (i, 0)
    return pl.pallas_call(
        _inproj_even_kernel,
        out_shape=(jax.ShapeDtypeStruct((t, NA_W), MXU_DTYPE),
                   jax.ShapeDtypeStruct((t, NA_W), MXU_DTYPE),
                   jax.ShapeDtypeStruct((t, NA_W), MXU_DTYPE),
                   jax.ShapeDtypeStruct((t, 2 * FNET_W), MXU_DTYPE)),
        grid=(t // tm,),
        in_specs=[pl.BlockSpec((tm, d), row),
                  pl.BlockSpec((1, 3, d), lambda i: (i // tpm, 0, 0)),
                  _const_spec((1, d)),
                  _const_spec(w_in.shape),
                  _const_spec((1, NA_W)),
                  _const_spec((1, NA_W)),
                  _const_spec(p256.shape),
                  _const_spec(csc.shape)],
        out_specs=(pl.BlockSpec((tm, NA_W), row),
                   pl.BlockSpec((tm, NA_W), row),
                   pl.BlockSpec((tm, NA_W), row),
                   pl.BlockSpec((tm, 2 * FNET_W), row)),
        compiler_params=_cparams(1),
        name="inproj_even",
    )(x, mod, g.reshape(1, d), w_in, gq, gk, p256, csc)


def _softmax_pv(scores, values):
    m = functools.reduce(jnp.maximum, [jnp.max(s, axis=-1, keepdims=True) for s in scores])
    ps = [jnp.exp(s - m) for s in scores]
    l = functools.reduce(jnp.add, [jnp.sum(p, axis=-1, keepdims=True) for p in ps])
    o = functools.reduce(jnp.add, [_dot(p.astype(MXU_DTYPE), v) for p, v in zip(ps, values)])
    return o * (1.0 / l)


def _na_kernel(q_ref, k_ref, v_ref, kc_ref, vc_ref, bias_ref, o_ref, *, n_rows):
    lo = lax.broadcasted_iota(jnp.int32, (1, LANES), 1) < HEAD_DIM
    kc = kc_ref[...]
    vc = vc_ref[...]
    n_steps = n_rows // NA_QROWS

    def step(jb, carry):
        r = jb * NA_QROWS
        u0 = jnp.clip(r - NA_WIN_ROWS // 2, 0, n_rows - NA_KROWS)
        cfg = jnp.where(jb == 0, 0, jnp.where(jb == n_steps - 1, 2, 1))
        qs = pl.multiple_of(r * GRID_W, NA_QROWS * GRID_W)
        ks = pl.multiple_of(u0 * GRID_W, NA_QROWS * GRID_W)
        q = q_ref[pl.ds(qs, NA_QROWS * GRID_W), :]
        kn = k_ref[pl.ds(ks, NA_KROWS * GRID_W), :]
        vn = v_ref[pl.ds(ks, NA_KROWS * GRID_W), :]
        outs = []
        for half in range(2):
            qm = jnp.where(lo if half == 0 else jnp.logical_not(lo), q, jnp.zeros_like(q))
            s_loc = _dot_nt(qm, kn) + bias_ref[0, half, cfg]
            s_ctx = _dot_nt(qm, kc)
            outs.append(_softmax_pv([s_loc, s_ctx], [vn, vc]))
        o_ref[pl.ds(qs, NA_QROWS * GRID_W), :] = jnp.where(lo, outs[0], outs[1]).astype(o_ref.dtype)
        return carry

    lax.fori_loop(0, n_steps, step, 0)


def _na_bias_tables(rpb, n_rows):
    h = rpb.shape[0]
    nq, nk = NA_QROWS * GRID_W, NA_KROWS * GRID_W
    qr, qc = np.divmod(np.arange(nq), GRID_W)
    ur, kcol = np.divmod(np.arange(nk), GRID_W)
    col_start = np.clip(qc - NA_WIN_COLS // 2, 0, GRID_W - NA_WIN_COLS)
    col_ok = (kcol[None, :] >= col_start[:, None]) & (kcol[None, :] < col_start[:, None] + NA_WIN_COLS)
    dcol = np.clip(kcol[None, :] - qc[:, None] + NA_WIN_COLS - 1, 0, 2 * NA_WIN_COLS - 2)
    drows, masks = [], []
    for r in (0, 2 * NA_QROWS, n_rows - NA_QROWS):
        u0 = int(np.clip(r - NA_WIN_ROWS // 2, 0, n_rows - NA_KROWS))
        row = r + qr
        row_start = np.clip(row - NA_WIN_ROWS // 2, 0, n_rows - NA_WIN_ROWS)
        key_row = u0 + ur
        row_ok = (key_row[None, :] >= row_start[:, None]) & (key_row[None, :] < row_start[:, None] + NA_WIN_ROWS)
        drows.append(np.clip(key_row[None, :] - row[:, None] + NA_WIN_ROWS - 1, 0, 2 * NA_WIN_ROWS - 2))
        masks.append(row_ok & col_ok)
    drow = np.stack(drows)
    mask = np.stack(masks)
    dcol3 = np.broadcast_to(dcol[None], drow.shape)
    bias = rpb.astype(F32)[:, drow, dcol3]
    bias = jnp.where(jnp.asarray(mask)[None], bias, NEG_INF)
    return bias.reshape(h // 2, 2, 3, nq, nk)


def _na_attention(q, k, v, kc, vc, bias, batch, seq, ctx_len):
    n_pairs = NA_W // LANES
    lat = lambda p, b: (b, p)
    return pl.pallas_call(
        functools.partial(_na_kernel, n_rows=seq // GRID_W),
        out_shape=jax.ShapeDtypeStruct(q.shape, MXU_DTYPE),
        grid=(n_pairs, batch),
        in_specs=[pl.BlockSpec((seq, LANES), lat),
                  pl.BlockSpec((seq, LANES), lat),
                  pl.BlockSpec((seq, LANES), lat),
                  pl.BlockSpec((ctx_len, LANES), lat),
                  pl.BlockSpec((ctx_len, LANES), lat),
                  pl.BlockSpec((1,) + bias.shape[1:], lambda p, b: (p, 0, 0, 0, 0))],
        out_specs=pl.BlockSpec((seq, LANES), lat),
        compiler_params=_cparams(2),
        name="neighbourhood_attention",
    )(q, k, v, kc, vc, bias)


def _pair_attn_kernel(*refs, n_src):
    q_ref, o_ref = refs[0], refs[-1]
    k_refs, v_refs = refs[1:1 + n_src], refs[1 + n_src:1 + 2 * n_src]
    lo = lax.broadcasted_iota(jnp.int32, (1, LANES), 1) < HEAD_DIM
    q = q_ref[...]
    ks = [r[...] for r in k_refs]
    vs = [r[...] for r in v_refs]
    outs = []
    for half in range(2):
        qm = jnp.where(lo if half == 0 else jnp.logical_not(lo), q, jnp.zeros_like(q))
        outs.append(_softmax_pv([_dot_nt(qm, k) for k in ks], vs))
    o_ref[...] = jnp.where(lo, outs[0], outs[1]).astype(o_ref.dtype)


def _pair_attention(q, kv_sources, batch, q_len, tq, shared_kv):
    n_pairs = q.shape[1] // LANES
    n_q = q_len // tq
    kv_map = (lambda b, p, i: (b, 0)) if shared_kv else (lambda b, p, i: (b, p))
    q_map = lambda b, p, i: (b * n_q + i, p)
    ks = [pl.BlockSpec((n, LANES), kv_map) for _, _, n in kv_sources]
    return pl.pallas_call(
        functools.partial(_pair_attn_kernel, n_src=len(kv_sources)),
        out_shape=jax.ShapeDtypeStruct(q.shape, MXU_DTYPE),
        grid=(batch, n_pairs, n_q),
        in_specs=[pl.BlockSpec((tq, LANES), q_map)] + ks + ks,
        out_specs=pl.BlockSpec((tq, LANES), q_map),
        compiler_params=_cparams(3),
        name="pair_attention",
    )(q, *[k for k, _, _ in kv_sources], *[v for _, v, _ in kv_sources])


def _fnet_kernel(ct_ref, st_ref, uv_ref, o_ref):
    uv = uv_ref[0]
    w = uv.shape[1] // 2
    o_ref[0] = (_dot(ct_ref[...], uv[:, :w]) + _dot(st_ref[...], uv[:, w:])).astype(o_ref.dtype)


def _fnet_positions(uv, ct, nst, tm):
    b, t, w2 = uv.shape
    return pl.pallas_call(
        _fnet_kernel,
        out_shape=jax.ShapeDtypeStruct((b, t, w2 // 2), MXU_DTYPE),
        grid=(t // tm, b),
        in_specs=[pl.BlockSpec((tm, t), lambda i, j: (i, 0)),
                  pl.BlockSpec((tm, t), lambda i, j: (i, 0)),
                  pl.BlockSpec((1, t, w2), lambda i, j: (j, 0, 0))],
        out_specs=pl.BlockSpec((1, tm, w2 // 2), lambda i, j: (j, i, 0)),
        compiler_params=_cparams(2),
        name="fnet_positions",
    )(ct, nst, uv)


def _outproj_kernel(x_ref, mod_ref, a_ref, b_ref, wa_ref, wb_ref, o_ref):
    y = _dot(a_ref[...], wa_ref[...]) + _dot(b_ref[...], wb_ref[...])
    o_ref[...] = x_ref[...] + mod_ref[0][2:3, :] * y


def _outproj(x, mod, a, b, wa, wb, rows_per_mod, tm):
    t, d = x.shape
    tpm = rows_per_mod // tm
    row = lambda i: (i, 0)
    return pl.pallas_call(
        _outproj_kernel,
        out_shape=jax.ShapeDtypeStruct((t, d), F32),
        grid=(t // tm,),
        in_specs=[pl.BlockSpec((tm, d), row),
                  pl.BlockSpec((1, 3, d), lambda i: (i // tpm, 0, 0)),
                  pl.BlockSpec((tm, a.shape[1]), row),
                  pl.BlockSpec((tm, b.shape[1]), row),
                  _const_spec(wa.shape),
                  _const_spec(wb.shape)],
        out_specs=pl.BlockSpec((tm, d), row),
        compiler_params=_cparams(1),
        name="outproj_residual",
    )(x, mod, a, b, wa, wb)


def _inproj_odd_kernel(x_ref, mod_ref, g_ref, w_ref, gq_ref, gk_ref, cos_ref, sin_ref, p_ref,
                       vg_ref, ws_ref, bs_ref, q_ref, k_ref, v_ref, sg_ref):
    h = _norm_modulate(x_ref[...], g_ref[...], mod_ref[0]).astype(MXU_DTYPE)
    z = _dot(h, w_ref[...])
    tm = z.shape[0]
    cos = cos_ref[...]
    sin = sin_ref[...]
    even = (lax.broadcasted_iota(jnp.int32, (1, LANES), 1) % 2) == 0

    def rope(t):
        swapped = jnp.where(even, pltpu.roll(t, LANES - 1, 1), pltpu.roll(t, 1, 1))
        return t * cos + swapped * sin

    for c in range(GQA_Q_W // 256):
        q = z[:, c * 256:(c + 1) * 256]
        qn = q * lax.rsqrt(_head_ms(q, p_ref) + EPS) * gq_ref[:, c * 256:(c + 1) * 256]
        for j in range(2):
            col = c * 256 + j * LANES
            q_ref[:, col:col + LANES] = rope(qn[:, j * LANES:(j + 1) * LANES]).astype(q_ref.dtype)
    k = z[:, GQA_Q_W:GQA_Q_W + GQA_KV_W]
    kn = k * lax.rsqrt(_head_ms(k, p_ref) + EPS) * gk_ref[...]
    k_ref[...] = rope(kn).astype(k_ref.dtype)
    v_ref[...] = z[:, GQA_Q_W + GQA_KV_W:GQA_Q_W + 2 * GQA_KV_W].astype(v_ref.dtype)

    o_u = GQA_Q_W + 2 * GQA_KV_W
    o_s = o_u + SGU_W
    for grp in range(SGU_GROUPS):
        sl = slice(grp * SGU_CHUNK, (grp + 1) * SGU_CHUNK)
        u = _gelu_tanh(z[:, o_u + grp * SGU_CHUNK:o_u + (grp + 1) * SGU_CHUNK])
        s = _gelu_tanh(z[:, o_s + grp * SGU_CHUNK:o_s + (grp + 1) * SGU_CHUNK])
        sn = s * lax.rsqrt(jnp.mean(s * s, axis=-1, keepdims=True) + EPS) * vg_ref[:, sl]
        sn = sn.astype(MXU_DTYPE)
        for ch in range(tm // SGU_CHUNK):
            rows = slice(ch * SGU_CHUNK, (ch + 1) * SGU_CHUNK)
            mixed = _dot(ws_ref[grp], sn[rows, :]) + bs_ref[grp]
            sg_ref[rows, sl] = (u[rows, :] * mixed).astype(sg_ref.dtype)


def _inproj_odd(x, mod, g, w_in, gq, gk, cos, sin, p256, vg, ws, bs, rows_per_mod, tm):
    t, d = x.shape
    tpm = rows_per_mod // tm
    n_pos = cos.shape[0] // tm
    row = lambda i: (i, 0)
    pos = lambda i: (i % n_pos, 0)
    return pl.pallas_call(
        _inproj_odd_kernel,
        out_shape=(jax.ShapeDtypeStruct((t, GQA_Q_W), MXU_DTYPE),
                   jax.ShapeDtypeStruct((t, GQA_KV_W), MXU_DTYPE),
                   jax.ShapeDtypeStruct((t, GQA_KV_W), MXU_DTYPE),
                   jax.ShapeDtypeStruct((t, SGU_W), MXU_DTYPE)),
        grid=(t // tm,),
        in_specs=[pl.BlockSpec((tm, d), row),
                  pl.BlockSpec((1, 3, d), lambda i: (i // tpm, 0, 0)),
                  _const_spec((1, d)),
                  _const_spec(w_in.shape),
                  _const_spec((1, GQA_Q_W)),
                  _const_spec((1, GQA_KV_W)),
                  pl.BlockSpec((tm, LANES), pos),
                  pl.BlockSpec((tm, LANES), pos),
                  _const_spec(p256.shape),
                  _const_spec((1, SGU_W)),
                  _const_spec(ws.shape),
                  _const_spec(bs.shape)],
        out_specs=(pl.BlockSpec((tm, GQA_Q_W), row),
                   pl.BlockSpec((tm, GQA_KV_W), row),
                   pl.BlockSpec((tm, GQA_KV_W), row),
                   pl.BlockSpec((tm, SGU_W), row)),
        compiler_params=_cparams(1),
        name="inproj_odd",
    )(x, mod, g.reshape(1, d), w_in, gq, gk, cos, sin, p256, vg, ws, bs)


def _dft_tables(n):
    j = jnp.arange(n, dtype=jnp.int32)
    ang = ((j[:, None] * j[None, :]) % n).astype(F32) * (2.0 * np.pi / n)
    return jnp.cos(ang), -jnp.sin(ang)


def _channel_dft(n_pos):
    c, ns = _dft_tables(FNET_GROUP_DIM)
    eye = jnp.eye(FNET_W // FNET_GROUP_DIM, dtype=F32)
    scale = 1.0 / np.sqrt(float(n_pos * FNET_GROUP_DIM))
    return jnp.concatenate([jnp.kron(eye, c), jnp.kron(eye, -ns)], axis=1) * scale


def _rope_tables(n_tokens):
    t = jnp.arange(n_tokens, dtype=jnp.int32)
    n_freq = HEAD_DIM // 4
    inv_freq = ROPE_THETA ** (-jnp.arange(n_freq, dtype=F32) / n_freq)
    row = (t // GRID_W).astype(F32)[:, None] * inv_freq
    col = (t % GRID_W).astype(F32)[:, None] * inv_freq
    ang = jnp.concatenate([row, col], axis=-1)
    cos = jnp.repeat(jnp.cos(ang), 2, axis=-1)
    sin = jnp.stack([-jnp.sin(ang), jnp.sin(ang)], axis=-1).reshape(n_tokens, HEAD_DIM)
    return jnp.tile(cos, (1, 2)), jnp.tile(sin, (1, 2))


def _head_mean_matrix():
    return jnp.kron(jnp.eye(256 // HEAD_DIM, dtype=F32),
                    jnp.full((HEAD_DIM, HEAD_DIM), 1.0 / HEAD_DIM, F32)).astype(MXU_DTYPE)


def kernel(x, c, ctx, c_ctx, norm_g, w_mod, b_mod, ffn_w_gu, ffn_w_down, w_in_ab, w_out_ab, qk_g_a,
           rpb_a, w_in_cd, w_out_cd, qk_g_d, v_g_c, w_s_c, b_s_c):
    b, s, d = x.shape
    n_ctx = ctx.shape[1]
    depth = w_mod.shape[0]
    tm = 256
    cast = lambda w: w.astype(MXU_DTYPE)

    xl = x.reshape(b * s, d)
    xc = ctx.reshape(b * n_ctx, d)

    pad = (-(b + 1)) % 8
    c_rows = jnp.concatenate([c, c_ctx[None, :], jnp.zeros((pad, d), F32)], axis=0)
    mod_all = _adaln_params(c_rows, w_mod, b_mod).reshape(depth, b + 1 + pad, N_SUB, 3, d)

    p256 = _head_mean_matrix()
    ct_l, nst_l = [cast(m) for m in _dft_tables(s)]
    ct_c, nst_c = [cast(m) for m in _dft_tables(n_ctx)]
    csc_l, csc_c = cast(_channel_dft(s)), cast(_channel_dft(n_ctx))
    cos_l, sin_l = _rope_tables(s)
    cos_c, sin_c = jnp.ones((n_ctx, LANES), F32), jnp.zeros((n_ctx, LANES), F32)
    q_perm = np.concatenate([np.r_[p * HEAD_DIM:(p + 1) * HEAD_DIM,
                                   (p + GQA_Q_HEADS // 2) * HEAD_DIM:(p + 1 + GQA_Q_HEADS // 2) * HEAD_DIM]
                             for p in range(GQA_Q_HEADS // 2)])

    for layer in range(depth):
        i = layer // 2
        with_ctx_out = layer < depth - 1
        ml = [mod_all[layer, :b, sub] for sub in range(N_SUB)]
        mc = [mod_all[layer, b:b + 1, sub] for sub in range(N_SUB)]
        wgu, wdn = cast(ffn_w_gu[layer]), cast(ffn_w_down[layer])

        xl = _ffn(xl, ml[0], norm_g[layer, 0], wgu[0], wdn[0], s, tm)
        xc = _ffn(xc, mc[0], norm_g[layer, 0], wgu[0], wdn[0], b * n_ctx, tm)

        if layer % 2 == 0:
            w_in, w_out = cast(w_in_ab[i]), cast(w_out_ab[i])
            gq = jnp.tile(qk_g_a[i, 0], NA_HEADS)[None, :] * ATTN_SCALE
            gk = jnp.tile(qk_g_a[i, 1], NA_HEADS)[None, :]
            ql, kl, vl, uvl = _inproj_even(xl, ml[1], norm_g[layer, 1], w_in, gq, gk, p256, csc_l, s, tm)
            qc, kc, vc, uvc = _inproj_even(xc, mc[1], norm_g[layer, 1], w_in, gq, gk, p256, csc_c,
                                           b * n_ctx, tm)
            bias = _na_bias_tables(rpb_a[i], s // GRID_W)
            att_l = _na_attention(ql, kl, vl, kc, vc, bias, b, s, n_ctx)
            mix_l = _fnet_positions(uvl.reshape(b, s, 2 * FNET_W), ct_l, nst_l, 512).reshape(b * s, FNET_W)
            wa, wb = w_out[:NA_W], w_out[NA_W:]
            if with_ctx_out:
                att_c = _pair_attention(qc, [(kc, vc, n_ctx)], b, n_ctx, n_ctx, shared_kv=False)
                mix_c = _fnet_positions(uvc.reshape(b, n_ctx, 2 * FNET_W), ct_c, nst_c,
                                        n_ctx).reshape(b * n_ctx, FNET_W)
        else:
            w_in = w_in_cd[i]
            w_in = cast(jnp.concatenate([w_in[:, :GQA_Q_W][:, q_perm], w_in[:, GQA_Q_W:]], axis=1))
            w_out = cast(w_out_cd[i])
            wa, wb = w_out[:GQA_Q_W][q_perm], w_out[GQA_Q_W:]
            gq = jnp.tile(qk_g_d[i, 0], GQA_Q_HEADS)[None, :] * ATTN_SCALE
            gk = jnp.tile(qk_g_d[i, 1], GQA_KV_HEADS)[None, :]
            vg = v_g_c[i][None, :]
            ws = cast(w_s_c[i])
            bs = jnp.broadcast_to(b_s_c[i][:, :, None], (SGU_GROUPS, SGU_CHUNK, SGU_CHUNK))
            ql, kl, vl, mix_l = _inproj_odd(xl, ml[1], norm_g[layer, 1], w_in, gq, gk, cos_l, sin_l, p256,
                                            vg, ws, bs, s, tm)
            qc, kc, vc, mix_c = _inproj_odd(xc, mc[1], norm_g[layer, 1], w_in, gq, gk, cos_c, sin_c, p256,
                                            vg, ws, bs, b * n_ctx, tm)
            att_l = _pair_attention(ql, [(kl, vl, s), (kc, vc, n_ctx)], b, s, 256, shared_kv=True)
            if with_ctx_out:
                att_c = _pair_attention(qc, [(kc, vc, n_ctx)], b, n_ctx, n_ctx, shared_kv=True)

        xl = _outproj(xl, ml[1], att_l, mix_l, wa, wb, s, tm)
        xl = _ffn(xl, ml[2], norm_g[layer, 2], wgu[1], wdn[1], s, tm)
        if with_ctx_out:
            xc = _outproj(xc, mc[1], att_c, mix_c, wa, wb, b * n_ctx, tm)
            xc = _ffn(xc, mc[2], norm_g[layer, 2], wgu[1], wdn[1], b * n_ctx, tm)

    return xl.reshape(b, s, d)
```
